```python
import jax, jax.numpy as jnp
from jax import lax
import numpy as np

D_MODEL = 2048
BATCH = 2
SEQ = 16384
DEPTH = 1

CHUNK = 64
D_CONV = D_MODEL // 2
CONV_KERNEL = 31
SB_HEADS = 8
SB_WIDTH = D_MODEL // 2
SB_HEAD_DIM = SB_WIDTH // SB_HEADS
Q_BLOCK = 128
D_FF = ((8 * D_MODEL // 3 + 255) // 256) * 256
N_SUB = 3
FFN_RES = 0.5
EPS = 1e-6
N_IN = 2 * D_CONV + 3 * SB_WIDTH + 2 * D_MODEL

kernel_name = "hybrid_conformer_stickbreaking_block"


def rmsnorm(x, g):
    xf = x.astype(jnp.float32)
    r = lax.rsqrt(jnp.mean(xf * xf, axis=-1, keepdims=True) + EPS)
    return (xf * r * g.astype(jnp.float32)).astype(x.dtype)


def layernorm(x, g, b):
    xf = x.astype(jnp.float32)
    mu = jnp.mean(xf, axis=-1, keepdims=True)
    xc = xf - mu
    var = jnp.mean(xc * xc, axis=-1, keepdims=True)
    y = xc * lax.rsqrt(var + EPS) * g.astype(jnp.float32) + b.astype(jnp.float32)
    return y.astype(x.dtype)


def swiglu_ffn(u, w_in, w_out):
    h = jnp.einsum('bsd,df->bsf', u, w_in)
    a, b = jnp.split(h, 2, axis=-1)
    return jnp.einsum('bsf,fd->bsd', jax.nn.silu(a) * b, w_out)


def conformer_conv_branch(h_in, dw, dw_b, ln_g, ln_b, w_pw):
    a, g = jnp.split(h_in, 2, axis=-1)
    h = a * jax.nn.sigmoid(g)
    h = lax.conv_general_dilated(
        h, dw[:, None, :].astype(h.dtype), window_strides=(1,),
        padding=[(CONV_KERNEL - 1, 0)],
        dimension_numbers=('NWC', 'WIO', 'NWC'),
        feature_group_count=D_CONV) + dw_b
    h = jax.nn.silu(layernorm(h, ln_g, ln_b))
    return jnp.einsum('bsc,cd->bsd', h, w_pw)


def stick_breaking_attention(q, k, v):
    B, S, H, dh = q.shape
    nb = S // Q_BLOCK
    def to_blocks(t):
        return t.astype(jnp.float32).reshape(B, nb, Q_BLOCK, H, dh).transpose(1, 0, 3, 2, 4)
    qb = to_blocks(q) * (dh ** -0.5)
    kb = to_blocks(k)
    vb = to_blocks(v)
    pos = jnp.arange(Q_BLOCK)

    def one_query_block(i):
        q_i = qb[i]
        t_pos = i * Q_BLOCK + pos
        def body(step, carry):
            out, acc = carry
            j = i - step
            z = jnp.einsum('bhtd,bhsd->bhts', q_i, kb[j])
            valid = (j * Q_BLOCK + pos)[None, :] < t_pos[:, None]
            log_1m = jnp.where(valid, jax.nn.log_sigmoid(-z), 0.0)
            rev = lax.cumsum(log_1m, axis=3, reverse=True)
            log_w = jax.nn.log_sigmoid(z) + (rev - log_1m) + acc[..., None]
            w = jnp.where(valid, jnp.exp(log_w), 0.0)
            out = out + jnp.einsum('bhts,bhsd->bhtd', w, vb[j])
            return out, acc + rev[..., 0]
        init = (jnp.zeros((B, H, Q_BLOCK, dh), jnp.float32),
                jnp.zeros((B, H, Q_BLOCK), jnp.float32))
        out, _ = lax.fori_loop(0, i + 1, body, init)
        return out

    out = lax.map(one_query_block, jnp.arange(nb))
    return out.transpose(1, 0, 3, 2, 4).reshape(B, S, H * dh).astype(q.dtype)


def token_mixer(u, w_in, conv_dw, conv_dw_b, conv_ln_g, conv_ln_b, conv_w_pw, sb_w_o, w_out):
    B, S, _ = u.shape
    proj = jnp.einsum('bsd,dn->bsn', u, w_in)
    cuts = np.cumsum([2 * D_CONV, SB_WIDTH, SB_WIDTH, SB_WIDTH, D_MODEL]).tolist()
    conv_in, q, k, v, g_a, g_b = jnp.split(proj, cuts, axis=-1)
    y_a = conformer_conv_branch(conv_in, conv_dw, conv_dw_b, conv_ln_g, conv_ln_b, conv_w_pw)
    shp = (B, S, SB_HEADS, SB_HEAD_DIM)
    o = stick_breaking_attention(q.reshape(shp), k.reshape(shp), v.reshape(shp))
    y_b = jnp.einsum('bsc,cd->bsd', o, sb_w_o)
    merged = jax.nn.sigmoid(g_a) * y_a + jax.nn.sigmoid(g_b) * y_b
    return jnp.einsum('bsd,de->bse', merged, w_out)


def sandwich(x, y_fn, g_pre, g_post, mod, res_w):
    shift, scale, gate = mod[:, 0, None, :], mod[:, 1, None, :], mod[:, 2, None, :]
    u = rmsnorm(x, g_pre) * (1.0 + scale) + shift
    y = rmsnorm(y_fn(u), g_post)
    return x + res_w * gate * y


def setup_inputs(seed: int = 0) -> dict:
    key = jax.random.key(seed)
    ks = jax.random.split(key, 24)
    f32 = jnp.float32
    def w(k, shape, fan_in):
        return jax.random.normal(k, shape, f32) * (fan_in ** -0.5)
    def gain(k, shape):
        return 1.0 + 0.01 * jax.random.normal(k, shape, f32)
    def bias(k, shape):
        return 0.01 * jax.random.normal(k, shape, f32)
    L, D = DEPTH, D_MODEL
    return {
        "x": jax.random.normal(ks[0], (BATCH, SEQ, D), f32),
        "c": jax.random.normal(ks[1], (BATCH, D), f32),
        "w_ada": w(ks[2], (L, D, N_SUB * 3 * D), D),
        "b_ada": bias(ks[3], (L, N_SUB * 3 * D)),
        "norm_pre": gain(ks[4], (L, N_SUB, D)),
        "norm_post": gain(ks[5], (L, N_SUB, D)),
        "norm_final": gain(ks[6], (L, D)),
        "ffn1_w_in": w(ks[7], (L, D, 2 * D_FF), D),
        "ffn1_w_out": w(ks[8], (L, D_FF, D), D_FF),
        "ffn2_w_in": w(ks[9], (L, D, 2 * D_FF), D),
        "ffn2_w_out": w(ks[10], (L, D_FF, D), D_FF),
        "w_in": w(ks[11], (L, D, N_IN), D),
        "conv_dw": w(ks[12], (L, CONV_KERNEL, D_CONV), CONV_KERNEL),
        "conv_dw_b": bias(ks[13], (L, D_CONV)),
        "conv_ln_g": gain(ks[14], (L, D_CONV)),
        "conv_ln_b": bias(ks[15], (L, D_CONV)),
        "conv_w_pw": w(ks[16], (L, D_CONV, D), D_CONV),
        "sb_w_o": w(ks[17], (L, SB_WIDTH, D), SB_WIDTH),
        "w_out": w(ks[18], (L, D, D), D),
    }


def reference(x, c, w_ada, b_ada, norm_pre, norm_post, norm_final, ffn1_w_in, ffn1_w_out,
              ffn2_w_in, ffn2_w_out, w_in, conv_dw, conv_dw_b, conv_ln_g, conv_ln_b,
              conv_w_pw, sb_w_o, w_out):
    B = x.shape[0]
    c_act = jax.nn.silu(c)
    for l in range(DEPTH):
        mod = (jnp.einsum('bd,dn->bn', c_act, w_ada[l]) + b_ada[l]).reshape(B, N_SUB, 3, D_MODEL)
        x = sandwich(x, lambda u: swiglu_ffn(u, ffn1_w_in[l], ffn1_w_out[l]),
                     norm_pre[l, 0], norm_post[l, 0], mod[:, 0], FFN_RES)
        x = sandwich(x, lambda u: token_mixer(u, w_in[l], conv_dw[l], conv_dw_b[l], conv_ln_g[l],
                                              conv_ln_b[l], conv_w_pw[l], sb_w_o[l], w_out[l]),
                     norm_pre[l, 1], norm_post[l, 1], mod[:, 1], 1.0)
        x = sandwich(x, lambda u: swiglu_ffn(u, ffn2_w_in[l], ffn2_w_out[l]),
                     norm_pre[l, 2], norm_post[l, 2], mod[:, 2], FFN_RES)
        x = rmsnorm(x, norm_final[l])
    return x
```

```python
import functools

import jax
import jax.numpy as jnp
from jax import lax
from jax.experimental import pallas as pl
from jax.experimental.pallas import tpu as pltpu

F32 = jnp.float32
BF16 = jnp.bfloat16

EPS = 1e-6
N_SUB = 3
SB_HEADS = 8
CONV_KERNEL = 31
FFN_RES = 0.5

CONV_PAD = 32

LOG_ZERO = -105.0

VMEM_LIMIT = 56 * 1024 * 1024


def _rms_scale(x):
    return lax.rsqrt(jnp.mean(x * x, axis=-1, keepdims=True) + EPS)


def _modnorm(x, g_pre, shift, scale):
    return (x * _rms_scale(x) * g_pre) * (1.0 + scale) + shift


def _sigmoid(x):
    return 1.0 / (1.0 + jnp.exp(-x))


def _adaln_kernel(c_ref, w_ref, b_ref, o_ref):
    c = c_ref[...]
    c_act = c * _sigmoid(c)
    c_hi = c_act.astype(BF16)
    c_lo = (c_act - c_hi.astype(F32)).astype(BF16)
    w = w_ref[...]
    w_hi = w.astype(BF16)
    w_lo = (w - w_hi.astype(F32)).astype(BF16)
    acc = jnp.dot(c_hi, w_hi, preferred_element_type=F32)
    acc += jnp.dot(c_lo, w_hi, preferred_element_type=F32)
    acc += jnp.dot(c_hi, w_lo, preferred_element_type=F32)
    o_ref[...] = acc + b_ref[...]


def _adaln(c8, w_ada, b_ada, tn=1024):
    d, n = w_ada.shape
    return pl.pallas_call(
        _adaln_kernel,
        grid=(n // tn,),
        in_specs=[
            pl.BlockSpec((8, d), lambda j: (0, 0)),
            pl.BlockSpec((d, tn), lambda j: (0, j)),
            pl.BlockSpec((1, tn), lambda j: (0, j)),
        ],
        out_specs=pl.BlockSpec((8, tn), lambda j: (0, j)),
        out_shape=jax.ShapeDtypeStruct((8, n), F32),
        compiler_params=pltpu.CompilerParams(
            dimension_semantics=("arbitrary",), vmem_limit_bytes=VMEM_LIMIT),
        name="adaln",
    )(c8, w_ada, b_ada)


def _ffn_kernel(x_ref, mod_ref, gpre_ref, gpost_ref, gfin_ref, wa_ref, wb_ref,
                wo_ref, o_ref, u_ref, acc_ref, *, sub, res_w, final_norm):
    f = pl.program_id(2)

    @pl.when(f == 0)
    def _():
        shift = mod_ref[3 * sub:3 * sub + 1, :]
        scale = mod_ref[3 * sub + 1:3 * sub + 2, :]
        u_ref[...] = _modnorm(x_ref[...], gpre_ref[...], shift, scale).astype(BF16)
        acc_ref[...] = jnp.zeros_like(acc_ref)

    u = u_ref[...]
    ha = jnp.dot(u, wa_ref[...], preferred_element_type=F32)
    hb = jnp.dot(u, wb_ref[...], preferred_element_type=F32)
    act = (ha * _sigmoid(ha)) * hb
    acc_ref[...] += jnp.dot(act.astype(BF16), wo_ref[...], preferred_element_type=F32)

    @pl.when(f == pl.num_programs(2) - 1)
    def _():
        y = acc_ref[...]
        gate = mod_ref[3 * sub + 2:3 * sub + 3, :]
        out = x_ref[...] + (res_w * gate) * (y * _rms_scale(y) * gpost_ref[...])
        if final_norm:
            out = out * _rms_scale(out) * gfin_ref[...]
        o_ref[...] = out


def _ffn(x, mod9, g_pre, g_post, g_fin, w_in, w_out, *, sub, res_w, final_norm,
         tm=512, tf=512):
    b, s, d = x.shape
    d_ff = w_out.shape[0]
    nf = d_ff // tf
    kern = functools.partial(_ffn_kernel, sub=sub, res_w=res_w, final_norm=final_norm)
    row = pl.BlockSpec((1, d), lambda bi, i, f: (0, 0))
    return pl.pallas_call(
        kern,
        grid=(b, s // tm, nf),
        in_specs=[
            pl.BlockSpec((None, tm, d), lambda bi, i, f: (bi, i, 0)),
            pl.BlockSpec((None, 3 * N_SUB, d), lambda bi, i, f: (bi, 0, 0)),
            row, row, row,
            pl.BlockSpec((d, tf), lambda bi, i, f: (0, f)),
            pl.BlockSpec((d, tf), lambda bi, i, f: (0, nf + f)),
            pl.BlockSpec((tf, d), lambda bi, i, f: (f, 0)),
        ],
        out_specs=pl.BlockSpec((None, tm, d), lambda bi, i, f: (bi, i, 0)),
        out_shape=jax.ShapeDtypeStruct((b, s, d), F32),
        scratch_shapes=[pltpu.VMEM((tm, d), BF16), pltpu.VMEM((tm, d), F32)],
        compiler_params=pltpu.CompilerParams(
            dimension_semantics=("arbitrary", "arbitrary", "arbitrary"),
            vmem_limit_bytes=VMEM_LIMIT),
        name="ffn%d" % sub,
    )(x, mod9, g_pre, g_post, g_fin, w_in, w_in, w_out)


def _proj_kernel(x_ref, mod_ref, gpre_ref, *rest, mode, sub, q_chunks, q_scale):
    if mode == "glu":
        wa_ref, wg_ref, o_ref, u_ref = rest
    else:
        w_ref, o_ref, u_ref = rest
    j = pl.program_id(2)

    @pl.when(j == 0)
    def _():
        shift = mod_ref[3 * sub:3 * sub + 1, :]
        scale = mod_ref[3 * sub + 1:3 * sub + 2, :]
        u_ref[...] = _modnorm(x_ref[...], gpre_ref[...], shift, scale).astype(BF16)

    u = u_ref[...]
    if mode == "glu":
        a = jnp.dot(u, wa_ref[...], preferred_element_type=F32)
        g = jnp.dot(u, wg_ref[...], preferred_element_type=F32)
        out = a * _sigmoid(g)
    elif mode == "qkv":
        p = jnp.dot(u, w_ref[...], preferred_element_type=F32)
        out = p * jnp.where(j < q_chunks, q_scale, 1.0)
    else:
        out = _sigmoid(jnp.dot(u, w_ref[...], preferred_element_type=F32))
    o_ref[...] = out.astype(o_ref.dtype)


def _proj(x, mod9, g_pre, weights, *, mode, sub, tm=512, tn=512, q_chunks=0,
          q_scale=1.0):
    b, s, d = x.shape
    n = weights[0].shape[1]
    kern = functools.partial(_proj_kernel, mode=mode, sub=sub, q_chunks=q_chunks,
                             q_scale=q_scale)
    w_spec = pl.BlockSpec((d, tn), lambda bi, i, j: (0, j))
    return pl.pallas_call(
        kern,
        grid=(b, s // tm, n // tn),
        in_specs=[
            pl.BlockSpec((None, tm, d), lambda bi, i, j: (bi, i, 0)),
            pl.BlockSpec((None, 3 * N_SUB, d), lambda bi, i, j: (bi, 0, 0)),
            pl.BlockSpec((1, d), lambda bi, i, j: (0, 0)),
        ] + [w_spec] * len(weights),
        out_specs=pl.BlockSpec((None, tm, tn), lambda bi, i, j: (bi, i, j)),
        out_shape=jax.ShapeDtypeStruct((b, s, n), BF16),
        scratch_shapes=[pltpu.VMEM((tm, d), BF16)],
        compiler_params=pltpu.CompilerParams(
            dimension_semantics=("arbitrary", "arbitrary", "arbitrary"),
            vmem_limit_bytes=VMEM_LIMIT),
        name="proj_" + mode,
    )(x, mod9, g_pre, *weights)


def _attn_kernel(q_ref, k_ref, v_ref, o_ref, *, blk):
    i = pl.program_id(2)
    q = q_ref[...]
    row = lax.broadcasted_iota(jnp.int32, (blk, blk), 0)
    col = lax.broadcasted_iota(jnp.int32, (blk, blk), 1)
    later = (row > col).astype(BF16)

    def block(j, acc, masked):
        start = pl.multiple_of(j * blk, blk)
        kj = k_ref[pl.ds(start, blk), :]
        vj = v_ref[pl.ds(start, blk), :]
        z = lax.dot_general(q, kj, (((1,), (1,)), ((), ())), preferred_element_type=F32)
        soft = jnp.log(1.0 + jnp.exp(-jnp.abs(z)))
        log_b = jnp.minimum(z, 0.0) - soft
        log_1m = -jnp.maximum(z, 0.0) - soft
        if masked:
            valid = col < row
            log_1m = jnp.where(valid, log_1m, 0.0)
        hi = log_1m.astype(BF16)
        lo = (log_1m - hi.astype(F32)).astype(BF16)
        suffix = (jnp.dot(hi, later, preferred_element_type=F32)
                  + jnp.dot(lo, later, preferred_element_type=F32))
        w = jnp.exp(log_b + suffix + acc)
        if masked:
            w = jnp.where(valid, w, 0.0)
        contrib = jnp.dot(w.astype(BF16), vj, preferred_element_type=F32)
        return contrib, acc + jnp.sum(log_1m, axis=1, keepdims=True)

    out, acc = block(i, jnp.zeros((blk, 1), F32), True)

    def cond(carry):
        j, _, _, live = carry
        return jnp.logical_and(j >= 0, live > LOG_ZERO)

    def body(carry):
        j, out, acc, _ = carry
        contrib, acc = block(j, acc, False)
        return j - 1, out + contrib, acc, jnp.max(acc)

    _, out, _, _ = lax.while_loop(cond, body, (i - 1, out, acc, jnp.max(acc)))
    o_ref[...] = out.astype(o_ref.dtype)


def _attention(qkv, *, heads, blk=128):
    b, s, n3 = qkv.shape
    dh = n3 // (3 * heads)
    kern = functools.partial(_attn_kernel, blk=blk)
    return pl.pallas_call(
        kern,
        grid=(b, heads, s // blk),
        in_specs=[
            pl.BlockSpec((None, blk, dh), lambda bi, h, i: (bi, i, h)),
            pl.BlockSpec((None, s, dh), lambda bi, h, i: (bi, 0, heads + h)),
            pl.BlockSpec((None, s, dh), lambda bi, h, i: (bi, 0, 2 * heads + h)),
        ],
        out_specs=pl.BlockSpec((None, blk, dh), lambda bi, h, i: (bi, i, h)),
        out_shape=jax.ShapeDtypeStruct((b, s, heads * dh), BF16),
        compiler_params=pltpu.CompilerParams(
            dimension_semantics=("arbitrary", "arbitrary", "arbitrary"),
            vmem_limit_bytes=VMEM_LIMIT),
        name="attention",
    )(qkv, qkv, qkv)


def _mixer_out_kernel(x_ref, mod_ref, gpost_ref, h_ref, o_ref, sga_ref, sgb_ref,
                      dw_ref, dwb_ref, lng_ref, lnb_ref, wpw_ref, wo_ref, wout_ref,
                      out_ref, hp_ref, *, sub, tm):
    i = pl.program_id(1)
    hist = CONV_KERNEL - 1

    @pl.when(i == 0)
    def _():
        hp_ref[0:CONV_PAD, :] = jnp.zeros((CONV_PAD, hp_ref.shape[1]), F32)

    hp_ref[CONV_PAD:CONV_PAD + tm, :] = h_ref[...].astype(F32)
    conv = jnp.zeros((tm, hp_ref.shape[1]), F32) + dwb_ref[...]
    for j in range(CONV_KERNEL):
        conv = conv + hp_ref[CONV_PAD - hist + j:CONV_PAD - hist + j + tm, :] * dw_ref[j:j + 1, :]
    hp_ref[0:CONV_PAD, :] = hp_ref[tm:tm + CONV_PAD, :]

    mu = jnp.mean(conv, axis=-1, keepdims=True)
    xc = conv - mu
    var = jnp.mean(xc * xc, axis=-1, keepdims=True)
    ln = xc * lax.rsqrt(var + EPS) * lng_ref[...] + lnb_ref[...]
    act = ln * _sigmoid(ln)
    y_a = jnp.dot(act.astype(BF16), wpw_ref[...], preferred_element_type=F32)
    y_b = jnp.dot(o_ref[...], wo_ref[...], preferred_element_type=F32)
    merged = sga_ref[...].astype(F32) * y_a + sgb_ref[...].astype(F32) * y_b
    y = jnp.dot(merged.astype(BF16), wout_ref[...], preferred_element_type=F32)
    gate = mod_ref[3 * sub + 2:3 * sub + 3, :]
    out_ref[...] = x_ref[...] + gate * (y * _rms_scale(y) * gpost_ref[...])


def _mixer_out(x, mod9, g_post, h, o, sg, dw, dw_b, ln_g, ln_b, w_pw, w_o, w_out,
               *, sub, tm=256):
    b, s, d = x.shape
    dc = h.shape[2]
    kern = functools.partial(_mixer_out_kernel, sub=sub, tm=tm)

    def const(shape):
        return pl.BlockSpec(shape, lambda bi, i: (0,) * len(shape))

    return pl.pallas_call(
        kern,
        grid=(b, s // tm),
        in_specs=[
            pl.BlockSpec((None, tm, d), lambda bi, i: (bi, i, 0)),
            pl.BlockSpec((None, 3 * N_SUB, d), lambda bi, i: (bi, 0, 0)),
            const((1, d)),
            pl.BlockSpec((None, tm, dc), lambda bi, i: (bi, i, 0)),
            pl.BlockSpec((None, tm, dc), lambda bi, i: (bi, i, 0)),
            pl.BlockSpec((None, tm, d), lambda bi, i: (bi, i, 0)),
            pl.BlockSpec((None, tm, d), lambda bi, i: (bi, i, 1)),
            const((CONV_KERNEL, dc)), const((1, dc)), const((1, dc)), const((1, dc)),
            const((dc, d)), const((dc, d)), const((d, d)),
        ],
        out_specs=pl.BlockSpec((None, tm, d), lambda bi, i: (bi, i, 0)),
        out_shape=jax.ShapeDtypeStruct((b, s, d), F32),
        scratch_shapes=[pltpu.VMEM((tm + CONV_PAD, dc), F32)],
        compiler_params=pltpu.CompilerParams(
            dimension_semantics=("arbitrary", "arbitrary"),
            vmem_limit_bytes=VMEM_LIMIT),
        name="mixer_out",
    )(x, mod9, g_post, h, o, sg, sg, dw, dw_b, ln_g, ln_b, w_pw, w_o, w_out)


def kernel(x, c, w_ada, b_ada, norm_pre, norm_post, norm_final, ffn1_w_in, ffn1_w_out,
           ffn2_w_in, ffn2_w_out, w_in, conv_dw, conv_dw_b, conv_ln_g, conv_ln_b,
           conv_w_pw, sb_w_o, w_out):
    b, s, d = x.shape
    depth = w_ada.shape[0]
    dc = conv_dw.shape[2]
    sbw = sb_w_o.shape[1]
    dh = sbw // SB_HEADS
    c8 = jnp.pad(c, ((0, 8 - b), (0, 0)))
    for l in range(depth):
        mod = _adaln(c8, w_ada[l], b_ada[l][None, :])
        mod9 = mod[:b].reshape(b, 3 * N_SUB, d)
        row = lambda a: a[None, :]
        x = _ffn(x, mod9, row(norm_pre[l, 0]), row(norm_post[l, 0]), row(norm_final[l]),
                 ffn1_w_in[l].astype(BF16), ffn1_w_out[l].astype(BF16),
                 sub=0, res_w=FFN_RES, final_norm=False)
        wl = w_in[l]
        g_pre = row(norm_pre[l, 1])
        h = _proj(x, mod9, g_pre, [wl[:, :dc].astype(BF16), wl[:, dc:2 * dc].astype(BF16)],
                  mode="glu", sub=1)
        qkv = _proj(x, mod9, g_pre, [wl[:, 2 * dc:2 * dc + 3 * sbw].astype(BF16)],
                    mode="qkv", sub=1, tn=1024, q_chunks=sbw // 1024,
                    q_scale=float(dh) ** -0.5)
        sg = _proj(x, mod9, g_pre, [wl[:, 2 * dc + 3 * sbw:].astype(BF16)],
                   mode="gate", sub=1, tn=1024)
        o = _attention(qkv, heads=SB_HEADS)
        x = _mixer_out(x, mod9, row(norm_post[l, 1]), h, o, sg, conv_dw[l],
                       row(conv_dw_b[l]), row(conv_ln_g[l]), row(conv_ln_b[l]),
                       conv_w_pw[l].astype(BF16), sb_w_o[l].astype(BF16),
                       w_out[l].astype(BF16), sub=1)
        x = _ffn(x, mod9, row(norm_pre[l, 2]), row(norm_post[l, 2]), row(norm_final[l]),
                 ffn2_w_in[l].astype(BF16), ffn2_w_out[l].astype(BF16),
                 sub=2, res_w=FFN_RES, final_norm=(True))
    return x
```
